```python
import math
import jax, jax.numpy as jnp
from jax import lax
import numpy as np

D_MODEL = 1024
BATCH = 4
SEQ = 4096
DEPTH = 2

HEAD_DIM = 64
MIX_WIDTH = D_MODEL
FOURIER_WIDTH = MIX_WIDTH // 2
N_HEADS = (MIX_WIDTH - FOURIER_WIDTH) // HEAD_DIM
N_KV_HEADS = 2
KV_GROUP = N_HEADS // N_KV_HEADS
WINDOW = 128
BLOCK = 128
ROPE_THETA = 10000.0
D_FF = 2816
CONV_WIDTH = 3
EPS = 1e-6
Q_COLS = N_HEADS * HEAD_DIM
KV_COLS = N_KV_HEADS * HEAD_DIM
IN_COLS = FOURIER_WIDTH + Q_COLS + 2 * KV_COLS

kernel_name = "hybrid_fourier_swa_convffn_encoder"


def rmsnorm(x, g):
    xf = x.astype(jnp.float32)
    y = xf * lax.rsqrt(jnp.mean(xf * xf, axis=-1, keepdims=True) + EPS)
    return (y * g.astype(jnp.float32)).astype(x.dtype)


def rope(x):
    s, d = x.shape[1], x.shape[-1]
    inv_freq = 1.0 / (ROPE_THETA ** (jnp.arange(0, d, 2, dtype=jnp.float32) / d))
    ang = jnp.arange(s, dtype=jnp.float32)[:, None] * inv_freq[None, :]
    cos = jnp.cos(ang)[None, :, None, :]
    sin = jnp.sin(ang)[None, :, None, :]
    xf = x.astype(jnp.float32)
    x1, x2 = xf[..., : d // 2], xf[..., d // 2:]
    return jnp.concatenate([x1 * cos - x2 * sin, x2 * cos + x1 * sin], axis=-1).astype(x.dtype)


def fourier_mix(u, w_f, b_f):
    f = jnp.fft.fft2(u.astype(jnp.float32), axes=(1, 2), norm="ortho").real.astype(u.dtype)
    return f @ w_f + b_f


def windowed_gqa(q, k, v, sink):
    b, s, _, d = q.shape
    nb = s // BLOCK
    qb = q.reshape(b, nb, BLOCK, N_KV_HEADS, KV_GROUP, d)
    pad = ((0, 0), (BLOCK, BLOCK), (0, 0), (0, 0))
    kp, vp = jnp.pad(k, pad), jnp.pad(v, pad)
    kb = jnp.concatenate([kp[:, i * BLOCK: i * BLOCK + s].reshape(b, nb, BLOCK, N_KV_HEADS, d) for i in range(3)], axis=2)
    vb = jnp.concatenate([vp[:, i * BLOCK: i * BLOCK + s].reshape(b, nb, BLOCK, N_KV_HEADS, d) for i in range(3)], axis=2)
    scores = jnp.einsum("bnqkgd,bnjkd->bnkgqj", qb, kb).astype(jnp.float32) / math.sqrt(d)
    blk = jnp.arange(nb)[:, None]
    qpos = blk * BLOCK + jnp.arange(BLOCK)[None, :]
    kpos = blk * BLOCK - BLOCK + jnp.arange(3 * BLOCK)[None, :]
    valid = (kpos[:, None, :] >= 0) & (kpos[:, None, :] < s) & (jnp.abs(qpos[:, :, None] - kpos[:, None, :]) <= WINDOW)
    scores = jnp.where(valid[None, :, None, None], scores, jnp.finfo(jnp.float32).min)
    sink_col = jnp.broadcast_to(sink.astype(jnp.float32).reshape(1, 1, N_KV_HEADS, KV_GROUP, 1, 1), scores.shape[:-1] + (1,))
    probs = jax.nn.softmax(jnp.concatenate([scores, sink_col], axis=-1), axis=-1)[..., :-1]
    out = jnp.einsum("bnkgqj,bnjkd->bnqkgd", probs.astype(v.dtype), vb)
    return out.reshape(b, s, N_HEADS * d)


def dwconv_centred(h, w, bias):
    hp = jnp.pad(h, ((0, 0), (1, 1), (0, 0)))
    return hp[:, :-2] * w[0] + hp[:, 1:-1] * w[1] + hp[:, 2:] * w[2] + bias


def setup_inputs(seed: int = 0) -> dict:
    key = jax.random.key(seed)
    ks = jax.random.split(key, 16)
    f32 = jnp.float32
    res_scale = (2.0 * DEPTH) ** -0.5

    def nrm(k, shape, scale):
        return jax.random.normal(k, shape, f32) * scale

    return {
        "x": nrm(ks[0], (BATCH, SEQ, D_MODEL), 1.0),
        "norm1": 1.0 + nrm(ks[1], (DEPTH, D_MODEL), 0.02),
        "w_in": nrm(ks[2], (DEPTH, D_MODEL, IN_COLS), D_MODEL ** -0.5),
        "w_fourier": nrm(ks[3], (DEPTH, FOURIER_WIDTH, FOURIER_WIDTH), FOURIER_WIDTH ** -0.5),
        "b_fourier": nrm(ks[4], (DEPTH, FOURIER_WIDTH), 0.02),
        "q_norm": 1.0 + nrm(ks[5], (DEPTH, HEAD_DIM), 0.02),
        "k_norm": 1.0 + nrm(ks[6], (DEPTH, HEAD_DIM), 0.02),
        "sink": nrm(ks[7], (DEPTH, N_HEADS), 0.5),
        "g_fourier_out": 1.0 + nrm(ks[8], (DEPTH, FOURIER_WIDTH), 0.02),
        "g_attn_out": 1.0 + nrm(ks[9], (DEPTH, Q_COLS), 0.02),
        "w_o": nrm(ks[10], (DEPTH, MIX_WIDTH, D_MODEL), MIX_WIDTH ** -0.5 * res_scale),
        "norm2": 1.0 + nrm(ks[11], (DEPTH, D_MODEL), 0.02),
        "w_up": nrm(ks[12], (DEPTH, D_MODEL, 2 * D_FF), D_MODEL ** -0.5),
        "conv_w": nrm(ks[13], (DEPTH, CONV_WIDTH, 2 * D_FF), CONV_WIDTH ** -0.5),
        "conv_b": nrm(ks[14], (DEPTH, 2 * D_FF), 0.02),
        "w_down": nrm(ks[15], (DEPTH, D_FF, D_MODEL), D_FF ** -0.5 * res_scale),
    }


def reference(x, norm1, w_in, w_fourier, b_fourier, q_norm, k_norm, sink, g_fourier_out, g_attn_out, w_o, norm2, w_up, conv_w, conv_b, w_down):
    b, s, _ = x.shape
    for l in range(DEPTH):
        h = rmsnorm(x, norm1[l])
        p = h @ w_in[l]
        u = p[..., :FOURIER_WIDTH]
        q = p[..., FOURIER_WIDTH:FOURIER_WIDTH + Q_COLS].reshape(b, s, N_HEADS, HEAD_DIM)
        k = p[..., FOURIER_WIDTH + Q_COLS:FOURIER_WIDTH + Q_COLS + KV_COLS].reshape(b, s, N_KV_HEADS, HEAD_DIM)
        v = p[..., FOURIER_WIDTH + Q_COLS + KV_COLS:].reshape(b, s, N_KV_HEADS, HEAD_DIM)

        y_f = fourier_mix(u, w_fourier[l], b_fourier[l])

        q = rope(rmsnorm(q, q_norm[l]))
        k = rope(rmsnorm(k, k_norm[l]))
        y_a = windowed_gqa(q, k, v, sink[l])

        mix = jnp.concatenate([rmsnorm(y_f, g_fourier_out[l]), rmsnorm(y_a, g_attn_out[l])], axis=-1)
        x = x + mix @ w_o[l]

        h = rmsnorm(x, norm2[l])
        up = dwconv_centred(h @ w_up[l], conv_w[l], conv_b[l])
        gate, val = up[..., :D_FF], up[..., D_FF:]
        x = x + (jax.nn.silu(gate) * val) @ w_down[l]
    return x
```

```python
import functools
import math

import jax
import jax.numpy as jnp
from jax import lax
from jax.experimental import pallas as pl
from jax.experimental.pallas import tpu as pltpu

HEAD_DIM = 64
N_KV_HEADS = 2
WINDOW = 128
BLOCK = 128
ROPE_THETA = 10000.0
EPS = 1e-6
LANES = 128
SUBLANES = 8
TWIDDLE_ROWS = 64
TOKEN_TILE = 512
VMEM_LIMIT = 56 * 1024 * 1024

F32 = jnp.float32
BF16 = jnp.bfloat16


def _dot(a, b):
    return jnp.dot(a, b, preferred_element_type=F32)


def _dot_nt(a, b):
    return lax.dot_general(a, b, (((1,), (1,)), ((), ())), preferred_element_type=F32)


def _split_bf16(v):
    hi = v.astype(BF16)
    lo = (v - hi.astype(F32)).astype(BF16)
    return hi, lo


def _rms_scale(xf, width):
    return lax.rsqrt(jnp.sum(xf * xf, axis=-1, keepdims=True) * (1.0 / width) + EPS)


def _params(sem):
    return pltpu.CompilerParams(dimension_semantics=sem, vmem_limit_bytes=VMEM_LIMIT)


def _const_spec(shape):
    nd = len(shape)
    return pl.BlockSpec(shape, lambda *_: (0,) * nd, pipeline_mode=pl.Buffered(1))


def _twiddle_kernel(base_c_ref, base_s_ref, blk_c_ref, blk_s_ref, cos_ref, nsin_ref):
    b = pl.program_id(0)
    c0 = blk_c_ref[pl.ds(b, 1), :]
    s0 = blk_s_ref[pl.ds(b, 1), :]
    bc = base_c_ref[...]
    bs = base_s_ref[...]
    cos_ref[...] = (c0 * bc - s0 * bs).astype(BF16)
    nsin_ref[...] = (-(s0 * bc + c0 * bs)).astype(BF16)


def _dft_tables(n, rows):
    m = (rows[:, None] * jnp.arange(n, dtype=jnp.int32)[None, :]) % n
    ang = m.astype(F32) * (2.0 * math.pi / n)
    return jnp.cos(ang), jnp.sin(ang)


def _seq_dft_matrices(s):
    r = TWIDDLE_ROWS
    nblk = s // r
    base_c, base_s = _dft_tables(s, jnp.arange(r, dtype=jnp.int32))
    blk_c, blk_s = _dft_tables(s, jnp.arange(nblk, dtype=jnp.int32) * r)
    return pl.pallas_call(
        _twiddle_kernel,
        grid=(nblk,),
        in_specs=[_const_spec((r, s)), _const_spec((r, s)),
                  _const_spec((nblk, s)), _const_spec((nblk, s))],
        out_specs=[pl.BlockSpec((r, s), lambda i: (i, 0)),
                   pl.BlockSpec((r, s), lambda i: (i, 0))],
        out_shape=[jax.ShapeDtypeStruct((s, s), BF16)] * 2,
        compiler_params=_params(("arbitrary",)),
        name="twiddle",
    )(base_c, base_s, blk_c, blk_s)


def _head_norm_rope(t, gain, cos, sin_signed, seg_ones):
    hi, lo = _split_bf16(t * t)
    ss = _dot(hi, seg_ones) + _dot(lo, seg_ones)
    tn = t * lax.rsqrt(ss * (1.0 / HEAD_DIM) + EPS) * gain
    lane = lax.broadcasted_iota(jnp.int32, tn.shape, 1)
    half = HEAD_DIM // 2
    rot = jnp.where(lane % HEAD_DIM < half,
                    pltpu.roll(tn, LANES - half, axis=1),
                    pltpu.roll(tn, half, axis=1))
    return tn * cos + rot * sin_signed


def _dup_heads(t):
    lane = lax.broadcasted_iota(jnp.int32, t.shape, 1)
    sw = pltpu.roll(t, HEAD_DIM, axis=1)
    first = lane < HEAD_DIM
    return jnp.where(first, t, sw), jnp.where(first, sw, t)


def _in_proj_kernel(fw, qc, x_ref, g1_ref, w_in_ref, cdft_ref, qg_ref, kg_ref,
                    cos_ref, sin_ref, seg_ref, ab_ref, q_ref, k_ref, v_ref):
    x = x_ref[0]
    d = x.shape[-1]
    h = (x * _rms_scale(x, d) * g1_ref[...]).astype(BF16)
    p = _dot(h, w_in_ref[...])
    u = p[:, :fw].astype(BF16)
    ab_ref[0] = _dot(u, cdft_ref[...]).astype(BF16)

    cos = cos_ref[...]
    sin = sin_ref[...]
    seg = seg_ref[...]
    q_scale = 1.0 / math.sqrt(HEAD_DIM)
    for c in range(qc // LANES):
        t = p[:, fw + c * LANES: fw + (c + 1) * LANES]
        r = _head_norm_rope(t, qg_ref[...], cos, sin, seg) * q_scale
        q_ref[0, :, c * LANES:(c + 1) * LANES] = r.astype(BF16)
    kr = _head_norm_rope(p[:, fw + qc: fw + qc + LANES], kg_ref[...], cos, sin, seg)
    k0, k1 = _dup_heads(kr)
    k_ref[0, :, :LANES] = k0.astype(BF16)
    k_ref[0, :, LANES:] = k1.astype(BF16)
    v0, v1 = _dup_heads(p[:, fw + qc + LANES: fw + qc + 2 * LANES])
    v_ref[0, :, :LANES] = v0.astype(BF16)
    v_ref[0, :, LANES:] = v1.astype(BF16)


def _in_proj(x, g1, w_in, cdft, qg, kg, cos_t, sin_t, seg, *, fw, qc, tm):
    b, s, d = x.shape
    in_cols = w_in.shape[1]
    kernel = functools.partial(_in_proj_kernel, fw, qc)
    tok = lambda w: pl.BlockSpec((1, tm, w), lambda bi, i: (bi, i, 0))
    return pl.pallas_call(
        kernel,
        grid=(b, s // tm),
        in_specs=[tok(d), _const_spec((1, d)), _const_spec((d, in_cols)),
                  _const_spec((fw, 2 * fw)), _const_spec((1, LANES)), _const_spec((1, LANES)),
                  pl.BlockSpec((tm, LANES), lambda bi, i: (i, 0)),
                  pl.BlockSpec((tm, LANES), lambda bi, i: (i, 0)),
                  _const_spec((LANES, LANES))],
        out_specs=[tok(2 * fw), tok(qc), tok(2 * LANES), tok(2 * LANES)],
        out_shape=[jax.ShapeDtypeStruct((b, s, 2 * fw), BF16),
                   jax.ShapeDtypeStruct((b, s, qc), BF16),
                   jax.ShapeDtypeStruct((b, s, 2 * LANES), BF16),
                   jax.ShapeDtypeStruct((b, s, 2 * LANES), BF16)],
        compiler_params=_params(("parallel", "parallel")),
        name="in_proj",
    )(x, g1, w_in, cdft, qg, kg, cos_t, sin_t, seg)


def _seq_dft_kernel(fw, cos_ref, nsin_ref, ab_ref, f_ref):
    a = ab_ref[0, :, :fw]
    bb = ab_ref[0, :, fw:]
    f_ref[0] = (_dot(cos_ref[...], a) + _dot(nsin_ref[...], bb)).astype(f_ref.dtype)


def _seq_dft(cos_m, nsin_m, ab, *, fw, tk):
    b, s, _ = ab.shape
    return pl.pallas_call(
        functools.partial(_seq_dft_kernel, fw),
        grid=(b, s // tk),
        in_specs=[pl.BlockSpec((tk, s), lambda bi, i: (i, 0)),
                  pl.BlockSpec((tk, s), lambda bi, i: (i, 0)),
                  pl.BlockSpec((1, s, 2 * fw), lambda bi, i: (bi, 0, 0))],
        out_specs=pl.BlockSpec((1, tk, fw), lambda bi, i: (bi, i, 0)),
        out_shape=jax.ShapeDtypeStruct((b, s, fw), BF16),
        compiler_params=_params(("parallel", "parallel")),
        name="seq_dft",
    )(cos_m, nsin_m, ab)


def _attention_kernel(nblk_step, s, q_ref, k_ref, v_ref, sink_ref, o_ref):
    j = pl.program_id(1)
    band = 3 * BLOCK
    row = lax.broadcasted_iota(jnp.int32, (2 * BLOCK, band), 0) % BLOCK
    col = lax.broadcasted_iota(jnp.int32, (2 * BLOCK, band), 1)
    delta = col - row
    lane = lax.broadcasted_iota(jnp.int32, (BLOCK, LANES), 1)
    first = lane < HEAD_DIM
    n_pairs = q_ref.shape[-1] // LANES
    pairs_per_kv = n_pairs // N_KV_HEADS
    for ib in range(nblk_step):
        q0 = (j * nblk_step + ib) * BLOCK
        start = jnp.clip(q0 - BLOCK, 0, s - band)
        start = pl.multiple_of(start, BLOCK)
        rel = start - q0
        valid = jnp.abs(delta + rel) <= WINDOW
        for pr in range(n_pairs):
            kh = pr // pairs_per_kv
            kk = k_ref[0, pl.ds(start, band), kh * LANES:(kh + 1) * LANES]
            vv = v_ref[0, pl.ds(start, band), kh * LANES:(kh + 1) * LANES]
            q2 = q_ref[0, ib * BLOCK:(ib + 1) * BLOCK, pr * LANES:(pr + 1) * LANES]
            zero = jnp.zeros_like(q2)
            lhs = jnp.concatenate([jnp.where(first, q2, zero), jnp.where(first, zero, q2)], axis=0)
            sc = _dot_nt(lhs, kk)
            sc = jnp.where(valid, sc, -jnp.inf)
            rowi = lax.broadcasted_iota(jnp.int32, (2 * BLOCK, 1), 0)
            snk = jnp.where(rowi < BLOCK, sink_ref[0, 2 * pr], sink_ref[0, 2 * pr + 1])
            m = jnp.maximum(jnp.max(sc, axis=-1, keepdims=True), snk)
            e = jnp.exp(sc - m)
            den = jnp.sum(e, axis=-1, keepdims=True) + jnp.exp(snk - m)
            pv = _dot(e.astype(BF16), vv) / den
            o_ref[0, ib * BLOCK:(ib + 1) * BLOCK, pr * LANES:(pr + 1) * LANES] = jnp.where(
                first, pv[:BLOCK], pv[BLOCK:]).astype(o_ref.dtype)


def _attention(q, k2, v2, sink, *, tq):
    b, s, qc = q.shape
    nblk_step = tq // BLOCK
    return pl.pallas_call(
        functools.partial(_attention_kernel, nblk_step, s),
        grid=(b, s // tq),
        in_specs=[pl.BlockSpec((1, tq, qc), lambda bi, i: (bi, i, 0)),
                  pl.BlockSpec((1, s, k2.shape[-1]), lambda bi, i: (bi, 0, 0)),
                  pl.BlockSpec((1, s, v2.shape[-1]), lambda bi, i: (bi, 0, 0)),
                  pl.BlockSpec(memory_space=pltpu.MemorySpace.SMEM)],
        out_specs=pl.BlockSpec((1, tq, qc), lambda bi, i: (bi, i, 0)),
        out_shape=jax.ShapeDtypeStruct((b, s, qc), BF16),
        compiler_params=_params(("parallel", "parallel")),
        name="attention",
    )(q, k2, v2, sink)


def _out_proj_kernel(x_ref, f_ref, ya_ref, wf_ref, bf_ref, gf_ref, ga_ref, wo_ref, o_ref):
    fw = f_ref.shape[-1]
    yf = _dot(f_ref[0], wf_ref[...]) + bf_ref[...]
    nf = (yf * _rms_scale(yf, fw) * gf_ref[...]).astype(BF16)
    ya = ya_ref[0].astype(F32)
    na = (ya * _rms_scale(ya, ya.shape[-1]) * ga_ref[...]).astype(BF16)
    o_ref[0] = x_ref[0] + _dot(nf, wo_ref[:fw, :]) + _dot(na, wo_ref[fw:, :])


def _out_proj(x, f, ya, wf, bf, gf, ga, wo, *, tm):
    b, s, d = x.shape
    fw = f.shape[-1]
    qc = ya.shape[-1]
    tok = lambda w: pl.BlockSpec((1, tm, w), lambda bi, i: (bi, i, 0))
    return pl.pallas_call(
        _out_proj_kernel,
        grid=(b, s // tm),
        in_specs=[tok(d), tok(fw), tok(qc), _const_spec((fw, fw)), _const_spec((1, fw)),
                  _const_spec((1, fw)), _const_spec((1, qc)), _const_spec((fw + qc, d))],
        out_specs=tok(d),
        out_shape=jax.ShapeDtypeStruct((b, s, d), F32),
        compiler_params=_params(("parallel", "parallel")),
        name="out_proj",
    )(x, f, ya, wf, bf, gf, ga, wo)


def _shift_rows(t, edge_prev, edge_next):
    rows = t.shape[0]
    sub = lax.broadcasted_iota(jnp.int32, (SUBLANES, t.shape[1]), 0)
    dn = pltpu.roll(t, 1, axis=0)
    dn = jnp.concatenate([jnp.where(sub == 0, edge_prev, dn[:SUBLANES]), dn[SUBLANES:]], axis=0)
    up = pltpu.roll(t, rows - 1, axis=0)
    up = jnp.concatenate([up[:rows - SUBLANES],
                          jnp.where(sub == SUBLANES - 1, edge_next, up[rows - SUBLANES:])], axis=0)
    return dn, up


def _conv_ffn_kernel(d_ff, cw, x_ref, xp_ref, xn_ref, g2_ref, wup_ref, cwt_ref, cb_ref,
                     wdn_ref, o_ref):
    i = pl.program_id(1)
    last = pl.num_programs(1) - 1
    x = x_ref[0]
    d = x.shape[-1]
    g2 = g2_ref[...]
    h = (x * _rms_scale(x, d) * g2).astype(BF16)
    xe = jnp.concatenate([xp_ref[0], xn_ref[0]], axis=0)
    erow = lax.broadcasted_iota(jnp.int32, (2 * SUBLANES, 1), 0)
    keep = jnp.where(erow < SUBLANES, (i > 0).astype(F32), (i < last).astype(F32))
    he = (xe * (_rms_scale(xe, d) * keep) * g2).astype(BF16)

    def conv(col):
        w = wup_ref[:, col:col + cw]
        t = _dot(h, w)
        te = _dot(he, w)
        dn, up = _shift_rows(t, te[SUBLANES - 1:SUBLANES], te[SUBLANES:SUBLANES + 1])
        cwt = cwt_ref[:, col:col + cw]
        return dn * cwt[0:1] + t * cwt[1:2] + up * cwt[2:3] + cb_ref[:, col:col + cw]

    acc = x
    for c in range(d_ff // cw):
        gate = conv(c * cw)
        val = conv(d_ff + c * cw)
        act = (gate / (1.0 + jnp.exp(-gate)) * val).astype(BF16)
        acc = acc + _dot(act, wdn_ref[c * cw:(c + 1) * cw, :])
    o_ref[0] = acc


def _conv_ffn(x, g2, wup, cwt, cb, wdn, *, ts, cw):
    b, s, d = x.shape
    d_ff = wdn.shape[0]
    nsub = ts // SUBLANES
    nsub_total = s // SUBLANES
    return pl.pallas_call(
        functools.partial(_conv_ffn_kernel, d_ff, cw),
        grid=(b, s // ts),
        in_specs=[pl.BlockSpec((1, ts, d), lambda bi, i: (bi, i, 0)),
                  pl.BlockSpec((1, SUBLANES, d),
                               lambda bi, i: (bi, jnp.maximum(i * nsub - 1, 0), 0)),
                  pl.BlockSpec((1, SUBLANES, d),
                               lambda bi, i: (bi, jnp.minimum((i + 1) * nsub, nsub_total - 1), 0)),
                  _const_spec((1, d)), _const_spec((d, 2 * d_ff)), _const_spec((3, 2 * d_ff)),
                  _const_spec((1, 2 * d_ff)), _const_spec((d_ff, d))],
        out_specs=pl.BlockSpec((1, ts, d), lambda bi, i: (bi, i, 0)),
        out_shape=jax.ShapeDtypeStruct((b, s, d), F32),
        compiler_params=_params(("parallel", "parallel")),
        name="conv_ffn",
    )(x, x, x, g2, wup, cwt, cb, wdn)


def _channel_dft(c):
    cc, cs = _dft_tables(c, jnp.arange(c, dtype=jnp.int32))
    return jnp.concatenate([cc, cs], axis=1)


def _rope_tables(s):
    half = HEAD_DIM // 2
    inv_freq = 1.0 / (ROPE_THETA ** (jnp.arange(0, HEAD_DIM, 2, dtype=F32) / HEAD_DIM))
    ang = jnp.arange(s, dtype=F32)[:, None] * inv_freq[None, :]
    cos, sin = jnp.cos(ang), jnp.sin(ang)
    reps = LANES // HEAD_DIM
    return (jnp.tile(jnp.concatenate([cos, cos], axis=1), (1, reps)),
            jnp.tile(jnp.concatenate([-sin, sin], axis=1), (1, reps)))


def kernel(x, norm1, w_in, w_fourier, b_fourier, q_norm, k_norm, sink, g_fourier_out, g_attn_out,
           w_o, norm2, w_up, conv_w, conv_b, w_down):
    b, s, d = x.shape
    depth = norm1.shape[0]
    fw = w_fourier.shape[-1]
    qc = g_attn_out.shape[-1]
    assert w_in.shape[-1] == fw + qc + 2 * N_KV_HEADS * HEAD_DIM and N_KV_HEADS * HEAD_DIM == LANES
    tm = min(TOKEN_TILE, s)

    cos_m, nsin_m = _seq_dft_matrices(s)
    cdft = (_channel_dft(fw) * (1.0 / math.sqrt(s * fw))).astype(BF16)
    cos_t, sin_t = _rope_tables(s)
    lane = jnp.arange(LANES)
    seg = (lane[:, None] // HEAD_DIM == lane[None, :] // HEAD_DIM).astype(BF16)
    reps = LANES // HEAD_DIM

    for l in range(depth):
        ab, q, k2, v2 = _in_proj(
            x, norm1[l][None], w_in[l].astype(BF16), cdft,
            jnp.tile(q_norm[l], reps)[None], jnp.tile(k_norm[l], reps)[None],
            cos_t, sin_t, seg, fw=fw, qc=qc, tm=tm)
        f = _seq_dft(cos_m, nsin_m, ab, fw=fw, tk=tm)
        ya = _attention(q, k2, v2, sink[l][None], tq=tm)
        x = _out_proj(x, f, ya, w_fourier[l].astype(BF16), b_fourier[l][None],
                      g_fourier_out[l][None], g_attn_out[l][None], w_o[l].astype(BF16), tm=tm)
        x = _conv_ffn(x, norm2[l][None], w_up[l].astype(BF16), conv_w[l], conv_b[l][None],
                      w_down[l].astype(BF16), ts=tm, cw=256)
    return x
```

```python
import functools
import math

import jax
import jax.numpy as jnp
from jax import lax
from jax.experimental import pallas as pl
from jax.experimental.pallas import tpu as pltpu

HEAD_DIM = 64
N_KV_HEADS = 2
WINDOW = 128
BLOCK = 128
ROPE_THETA = 10000.0
EPS = 1e-6
LOG2E = math.log2(math.e)
LANES = 128
SUBLANES = 8
TWIDDLE_ROWS = 64
TOKEN_TILE = 512
SCORE_LOOKAHEAD = 4
VMEM_LIMIT = 56 * 1024 * 1024

F32 = jnp.float32
BF16 = jnp.bfloat16


def _dot(a, b):
    return jnp.dot(a, b, preferred_element_type=F32)


def _dot_nt(a, b):
    return lax.dot_general(a, b, (((1,), (1,)), ((), ())), preferred_element_type=F32)


def _rms_scale(xf, width):
    return lax.rsqrt(jnp.sum(xf * xf, axis=-1, keepdims=True) * (1.0 / width) + EPS)


def _params(sem):
    return pltpu.CompilerParams(dimension_semantics=sem, vmem_limit_bytes=VMEM_LIMIT)


def _const_spec(shape):
    nd = len(shape)
    return pl.BlockSpec(shape, lambda *_: (0,) * nd, pipeline_mode=pl.Buffered(1))


def _layer_spec(shape, layer):
    nd = len(shape)
    return pl.BlockSpec((None,) + tuple(shape), lambda *_: (layer,) + (0,) * nd,
                        pipeline_mode=pl.Buffered(1))


RADIX = 4


def _twiddle_kernel(base_c_ref, base_s_ref, blk_c_ref, blk_s_ref, tw_c_ref, tw_s_ref, out_ref):
    b = pl.program_id(0)
    n4 = base_c_ref.shape[1]
    c0 = blk_c_ref[pl.ds(b, 1), :]
    s0 = blk_s_ref[pl.ds(b, 1), :]
    bc = base_c_ref[...]
    bs = base_s_ref[...]
    c4 = c0 * bc - s0 * bs
    s4 = s0 * bc + c0 * bs
    for j in range(RADIX):
        cj = tw_c_ref[j:j + 1, :]
        sj = tw_s_ref[j:j + 1, :]
        out_ref[j, :, :n4] = (c4 * cj - s4 * sj).astype(BF16)
        out_ref[j, :, n4:] = (-(s4 * cj + c4 * sj)).astype(BF16)


def _dft_tables(n, rows, cols):
    m = (rows[:, None] * jnp.arange(cols, dtype=jnp.int32)[None, :]) % n
    ang = m.astype(F32) * (2.0 * math.pi / n)
    return jnp.cos(ang), jnp.sin(ang)


def _seq_dft_matrices(s):
    r = TWIDDLE_ROWS
    n4 = s // RADIX
    nblk = n4 // r
    base_c, base_s = _dft_tables(n4, jnp.arange(r, dtype=jnp.int32), n4)
    blk_c, blk_s = _dft_tables(n4, jnp.arange(nblk, dtype=jnp.int32) * r, n4)
    tw_c, tw_s = _dft_tables(s, jnp.arange(RADIX, dtype=jnp.int32), n4)
    return pl.pallas_call(
        _twiddle_kernel,
        grid=(nblk,),
        in_specs=[_const_spec((r, n4)), _const_spec((r, n4)),
                  _const_spec((nblk, n4)), _const_spec((nblk, n4)),
                  _const_spec((RADIX, n4)), _const_spec((RADIX, n4))],
        out_specs=pl.BlockSpec((RADIX, r, 2 * n4), lambda i: (0, i, 0)),
        out_shape=jax.ShapeDtypeStruct((RADIX, n4, 2 * n4), BF16),
        compiler_params=_params(("arbitrary",)),
        name="twiddle",
    )(base_c, base_s, blk_c, blk_s, tw_c, tw_s)


def _head_meansq(t, seg_mean):
    return _dot((t * t).astype(BF16), seg_mean)


def _head_norm_rope(t, ms, gain, cos, sin_signed):
    tn = t * lax.rsqrt(ms + EPS) * gain
    lane = lax.broadcasted_iota(jnp.int32, tn.shape, 1)
    half = HEAD_DIM // 2
    rot = jnp.where(lane % HEAD_DIM < half,
                    pltpu.roll(tn, LANES - half, axis=1),
                    pltpu.roll(tn, half, axis=1))
    return tn * cos + rot * sin_signed


def _split_heads(t, fill):
    lane = lax.broadcasted_iota(jnp.int32, t.shape, 1)
    sw = pltpu.roll(t, HEAD_DIM, axis=1)
    first = lane < HEAD_DIM
    if fill is None:
        return jnp.where(first, t, sw), jnp.where(first, sw, t)
    return jnp.where(first, t, fill), jnp.where(first, sw, fill)


def _in_proj_kernel(fw, qc, sub, x_ref, g1_ref, w_in_ref, cdft_ref, qg_ref, kg_ref,
                    cos_ref, sin_ref, seg_ref, ab_ref, q_ref, k_ref, v_ref):
    d = x_ref.shape[-1]
    seg = seg_ref[...]
    sw = seg.shape[0]

    def project(rows):
        x = x_ref[0, rows, :]
        h = (x * _rms_scale(x, d) * g1_ref[...]).astype(BF16)
        return _dot(h, w_in_ref[...])

    def finish(rows, p):
        ab_ref[0, rows, :] = _dot(p[:, :fw].astype(BF16), cdft_ref[...]).astype(BF16)
        cos = cos_ref[rows, :]
        sin = sin_ref[rows, :]
        for c0 in range(0, qc, sw):
            ms = _head_meansq(p[:, fw + c0: fw + c0 + sw], seg)
            for c in range(c0, c0 + sw, LANES):
                t = p[:, fw + c: fw + c + LANES]
                r = _head_norm_rope(t, ms[:, c - c0:c - c0 + LANES], qg_ref[...], cos, sin)
                q_ref[0, rows, c:c + LANES] = r.astype(BF16)
        kcol = p[:, fw + qc: fw + qc + LANES]
        kr = _head_norm_rope(kcol, _head_meansq(kcol, seg[:LANES, :LANES]), kg_ref[...], cos, sin)
        k0, k1 = _split_heads(kr, None)
        k_ref[0, rows, :LANES] = k0.astype(BF16)
        k_ref[0, rows, LANES:] = k1.astype(BF16)
        v0, v1 = _split_heads(p[:, fw + qc + LANES: fw + qc + 2 * LANES], 1.0)
        v_ref[0, rows, :LANES] = v0.astype(BF16)
        v_ref[0, rows, LANES:] = v1.astype(BF16)

    subs = [pl.ds(r0, sub) for r0 in range(0, x_ref.shape[1], sub)]
    p = project(subs[0])
    for idx, rows in enumerate(subs):
        nxt = project(subs[idx + 1]) if idx + 1 < len(subs) else None
        finish(rows, p)
        p = nxt


def _in_proj(x, g1, w_in, cdft, qg, kg, cos_t, sin_t, seg, *, layer, fw, qc, tm, sub):
    b, s, d = x.shape
    in_cols = w_in.shape[-1]
    kernel = functools.partial(_in_proj_kernel, fw, qc, sub)
    tok = lambda w: pl.BlockSpec((1, tm, w), lambda bi, i: (bi, i, 0))
    return pl.pallas_call(
        kernel,
        grid=(b, s // tm),
        in_specs=[tok(d), _layer_spec((1, d), layer), _layer_spec((d, in_cols), layer),
                  _const_spec((fw, 2 * fw)), _layer_spec((1, LANES), layer),
                  _layer_spec((1, LANES), layer),
                  pl.BlockSpec((tm, LANES), lambda bi, i: (i, 0)),
                  pl.BlockSpec((tm, LANES), lambda bi, i: (i, 0)),
                  _const_spec(seg.shape)],
        out_specs=[tok(2 * fw), tok(qc), tok(2 * LANES), tok(2 * LANES)],
        out_shape=[jax.ShapeDtypeStruct((b, s, 2 * fw), BF16),
                   jax.ShapeDtypeStruct((b, s, qc), BF16),
                   jax.ShapeDtypeStruct((b, s, 2 * LANES), BF16),
                   jax.ShapeDtypeStruct((b, s, 2 * LANES), BF16)],
        compiler_params=_params(("parallel", "parallel")),
        name="in_proj",
    )(x, g1, w_in, cdft, qg, kg, cos_t, sin_t, seg)


def _seq_dft_kernel(fw, rc, tw_ref, ab_ref, f_ref, pr_ref):
    j = pl.program_id(1)
    n4 = tw_ref.shape[1]

    @pl.when(j == 0)
    def _():
        def body(r, carry):
            def ld(q, lo):
                return ab_ref[0, pl.ds(pl.multiple_of(q * n4 + r * rc, rc), rc), lo:lo + fw].astype(F32)
            a0, a1, a2, a3 = (ld(q, 0) for q in range(RADIX))
            b0, b1, b2, b3 = (ld(q, fw) for q in range(RADIX))
            sa02, da02, sa13, da13 = a0 + a2, a0 - a2, a1 + a3, a1 - a3
            sb02, db02, sb13, db13 = b0 + b2, b0 - b2, b1 + b3, b1 - b3
            p = (sa02 + sa13, da02 - db13, sa02 - sa13, da02 + db13)
            rr = (sb02 + sb13, db02 + da13, sb02 - sb13, db02 - da13)
            rows = pl.ds(pl.multiple_of(r * rc, rc), rc)
            rows_r = pl.ds(pl.multiple_of(n4 + r * rc, rc), rc)
            for jj in range(RADIX):
                pr_ref[jj, rows, :] = p[jj].astype(BF16)
                pr_ref[jj, rows_r, :] = rr[jj].astype(BF16)
            return carry
        lax.fori_loop(0, n4 // rc, body, 0)

    f_ref[0] = _dot(tw_ref[0], pr_ref[j]).astype(f_ref.dtype)


def _seq_dft(tw, ab, *, fw):
    b, s, _ = ab.shape
    n4 = s // RADIX
    rc = min(128, n4)
    f = pl.pallas_call(
        functools.partial(_seq_dft_kernel, fw, rc),
        grid=(b, RADIX),
        in_specs=[pl.BlockSpec((1, n4, 2 * n4), lambda bi, j: (j, 0, 0)),
                  pl.BlockSpec((1, s, 2 * fw), lambda bi, j: (bi, 0, 0))],
        out_specs=pl.BlockSpec((1, n4, fw), lambda bi, j: (bi, 0, j)),
        out_shape=jax.ShapeDtypeStruct((b, n4, RADIX * fw), BF16),
        scratch_shapes=[pltpu.VMEM((RADIX, 2 * n4, fw), BF16)],
        compiler_params=_params(("parallel", "arbitrary")),
        name="seq_dft",
    )(tw, ab)
    return f.reshape(b, s, fw)


def _attention_kernel(layer, nblk_step, s, q_ref, k_ref, v_ref, bias_ref, sink_ref, o_ref):
    j = pl.program_id(1)
    band = 3 * BLOCK
    lane = lax.broadcasted_iota(jnp.int32, (BLOCK, LANES), 1)
    first = lane < HEAD_DIM
    n_pairs = q_ref.shape[-1] // LANES
    pairs_per_kv = n_pairs // N_KV_HEADS

    def band_start(ib):
        q0 = (j * nblk_step + ib) * BLOCK
        return q0, pl.multiple_of(jnp.clip(q0 - BLOCK, 0, s - band), BLOCK)

    def scores(ib, pr):
        q0, start = band_start(ib)
        kh = pr // pairs_per_kv
        kk = k_ref[0, pl.ds(start, band), kh * LANES:(kh + 1) * LANES]
        q2 = q_ref[0, ib * BLOCK:(ib + 1) * BLOCK, pr * LANES:(pr + 1) * LANES]
        zero = jnp.zeros_like(q2)
        lhs = jnp.concatenate([jnp.where(first, q2, zero), jnp.where(first, zero, q2)], axis=0)
        bias = bias_ref[(q0 - start) // BLOCK]
        return _dot_nt(lhs, kk) + jnp.concatenate([bias, bias], axis=0)

    def finish(ib, pr, sc):
        _, start = band_start(ib)
        kh = pr // pairs_per_kv
        vv = v_ref[0, pl.ds(start, band), kh * LANES:(kh + 1) * LANES]
        mx = jnp.max(sc, axis=-1, keepdims=True)
        probs, sink_p = [], []
        for hh in range(2):
            snk = sink_ref[layer, 2 * pr + hh] * LOG2E
            m = jnp.maximum(mx[hh * BLOCK:(hh + 1) * BLOCK], snk)
            probs.append(jnp.exp2(sc[hh * BLOCK:(hh + 1) * BLOCK] - m).astype(BF16))
            sink_p.append(jnp.exp2(snk - m))
        pv = _dot(jnp.concatenate(probs, axis=0), vv)
        pa, pb = pv[:BLOCK], pv[BLOCK:]
        ra, rb = pltpu.roll(pa, HEAD_DIM, axis=1), pltpu.roll(pb, HEAD_DIM, axis=1)
        out_a = pa / (ra + sink_p[0])
        out_b = rb / (pb + sink_p[1])
        o_ref[0, ib * BLOCK:(ib + 1) * BLOCK, pr * LANES:(pr + 1) * LANES] = jnp.where(
            first, out_a, out_b).astype(o_ref.dtype)

    work = [(ib, pr) for ib in range(nblk_step) for pr in range(n_pairs)]
    pending = [scores(*item) for item in work[:SCORE_LOOKAHEAD]]
    for idx, item in enumerate(work):
        if idx + SCORE_LOOKAHEAD < len(work):
            pending.append(scores(*work[idx + SCORE_LOOKAHEAD]))
        finish(*item, pending.pop(0))


def _band_bias():
    i = jnp.arange(BLOCK)[None, :, None]
    jj = jnp.arange(3 * BLOCK)[None, None, :]
    off = jnp.arange(3)[:, None, None] * BLOCK
    return jnp.where(jnp.abs(jj - off - i) <= WINDOW, 0.0, -jnp.inf).astype(F32)


def _attention(q, k2, v2, bias, sink, *, layer, tq):
    b, s, qc = q.shape
    nblk_step = tq // BLOCK
    return pl.pallas_call(
        functools.partial(_attention_kernel, layer, nblk_step, s),
        grid=(b, s // tq),
        in_specs=[pl.BlockSpec((1, tq, qc), lambda bi, i: (bi, i, 0)),
                  pl.BlockSpec((1, s, k2.shape[-1]), lambda bi, i: (bi, 0, 0)),
                  pl.BlockSpec((1, s, v2.shape[-1]), lambda bi, i: (bi, 0, 0)),
                  _const_spec(bias.shape),
                  pl.BlockSpec(memory_space=pltpu.MemorySpace.SMEM)],
        out_specs=pl.BlockSpec((1, tq, qc), lambda bi, i: (bi, i, 0)),
        out_shape=jax.ShapeDtypeStruct((b, s, qc), BF16),
        compiler_params=_params(("parallel", "parallel")),
        name="attention",
    )(q, k2, v2, bias, sink)


def _out_proj_kernel(x_ref, f_ref, ya_ref, wf_ref, bf_ref, gf_ref, ga_ref, wo_ref, o_ref):
    fw = f_ref.shape[-1]
    yf = _dot(f_ref[0], wf_ref[...]) + bf_ref[...]
    nf = (yf * _rms_scale(yf, fw) * gf_ref[...]).astype(BF16)
    ya = ya_ref[0].astype(F32)
    na = (ya * _rms_scale(ya, ya.shape[-1]) * ga_ref[...]).astype(BF16)
    o_ref[0] = x_ref[0] + _dot(nf, wo_ref[:fw, :]) + _dot(na, wo_ref[fw:, :])


def _out_proj(x, f, ya, wf, bf, gf, ga, wo, *, layer, tm):
    b, s, d = x.shape
    fw = f.shape[-1]
    qc = ya.shape[-1]
    tok = lambda w: pl.BlockSpec((1, tm, w), lambda bi, i: (bi, i, 0))
    return pl.pallas_call(
        _out_proj_kernel,
        grid=(b, s // tm),
        in_specs=[tok(d), tok(fw), tok(qc), _layer_spec((fw, fw), layer),
                  _layer_spec((1, fw), layer), _layer_spec((1, fw), layer),
                  _layer_spec((1, qc), layer), _layer_spec((fw + qc, d), layer)],
        out_specs=tok(d),
        out_shape=jax.ShapeDtypeStruct((b, s, d), F32),
        compiler_params=_params(("parallel", "parallel")),
        name="out_proj",
    )(x, f, ya, wf, bf, gf, ga, wo)


def _shift_rows(t, edge_prev, edge_next):
    rows = t.shape[0]
    sub = lax.broadcasted_iota(jnp.int32, (SUBLANES, t.shape[1]), 0)
    dn = pltpu.roll(t, 1, axis=0)
    dn = jnp.concatenate([jnp.where(sub == 0, edge_prev, dn[:SUBLANES]), dn[SUBLANES:]], axis=0)
    up = pltpu.roll(t, rows - 1, axis=0)
    up = jnp.concatenate([up[:rows - SUBLANES],
                          jnp.where(sub == SUBLANES - 1, edge_next, up[rows - SUBLANES:])], axis=0)
    return dn, up


def _conv_ffn_kernel(d_ff, cw, x_ref, xp_ref, xn_ref, g2_ref, wup_ref, cwt_ref, cb_ref,
                     wdn_ref, o_ref, hext_ref, act_ref):
    i = pl.program_id(1)
    last = pl.num_programs(1) - 1
    x = x_ref[0]
    ts, d = x.shape
    g2 = g2_ref[...]
    hext_ref[:ts, :] = (x * _rms_scale(x, d) * g2).astype(BF16)
    xe = jnp.concatenate([xp_ref[0], xn_ref[0]], axis=0)
    erow = lax.broadcasted_iota(jnp.int32, (2 * SUBLANES, 1), 0)
    keep = jnp.where(erow < SUBLANES, (i > 0).astype(F32), (i < last).astype(F32))
    hext_ref[ts:, :] = (xe * (_rms_scale(xe, d) * keep) * g2).astype(BF16)

    def conv(col):
        r = _dot(hext_ref[...], wup_ref[:, col:col + cw])
        t = r[:ts]
        dn, up = _shift_rows(t, r[ts + SUBLANES - 1:ts + SUBLANES], r[ts + SUBLANES:ts + SUBLANES + 1])
        cwt = cwt_ref[:, col:col + cw]
        return dn * cwt[0:1] + t * cwt[1:2] + up * cwt[2:3] + cb_ref[:, col:col + cw]

    for c in range(d_ff // cw):
        gate = conv(c * cw)
        val = conv(d_ff + c * cw)
        act_ref[:, c * cw:(c + 1) * cw] = (gate / (1.0 + jnp.exp(-gate)) * val).astype(BF16)
    o_ref[0] = x + _dot(act_ref[...], wdn_ref[...])


def _conv_ffn(x, g2, wup, cwt, cb, wdn, *, layer, ts, cw):
    b, s, d = x.shape
    d_ff = wdn.shape[-2]
    nsub = ts // SUBLANES
    nsub_total = s // SUBLANES
    return pl.pallas_call(
        functools.partial(_conv_ffn_kernel, d_ff, cw),
        grid=(b, s // ts),
        in_specs=[pl.BlockSpec((1, ts, d), lambda bi, i: (bi, i, 0)),
                  pl.BlockSpec((1, SUBLANES, d),
                               lambda bi, i: (bi, jnp.maximum(i * nsub - 1, 0), 0)),
                  pl.BlockSpec((1, SUBLANES, d),
                               lambda bi, i: (bi, jnp.minimum((i + 1) * nsub, nsub_total - 1), 0)),
                  _layer_spec((1, d), layer), _layer_spec((d, 2 * d_ff), layer),
                  _layer_spec((cwt.shape[-2], 2 * d_ff), layer), _layer_spec((1, 2 * d_ff), layer),
                  _layer_spec((d_ff, d), layer)],
        out_specs=pl.BlockSpec((1, ts, d), lambda bi, i: (bi, i, 0)),
        out_shape=jax.ShapeDtypeStruct((b, s, d), F32),
        scratch_shapes=[pltpu.VMEM((ts + 2 * SUBLANES, d), BF16), pltpu.VMEM((ts, d_ff), BF16)],
        compiler_params=_params(("parallel", "parallel")),
        name="conv_ffn",
    )(x, x, x, g2, wup, cwt, cb, wdn)


def _channel_dft(c):
    cc, cs = _dft_tables(c, jnp.arange(c, dtype=jnp.int32), c)
    return jnp.concatenate([cc, cs], axis=1)


def _rope_tables(s):
    inv_freq = 1.0 / (ROPE_THETA ** (jnp.arange(0, HEAD_DIM, 2, dtype=F32) / HEAD_DIM))
    ang = jnp.arange(s, dtype=F32)[:, None] * inv_freq[None, :]
    cos, sin = jnp.cos(ang), jnp.sin(ang)
    reps = LANES // HEAD_DIM
    return (jnp.tile(jnp.concatenate([cos, cos], axis=1), (1, reps)),
            jnp.tile(jnp.concatenate([-sin, sin], axis=1), (1, reps)))


def kernel(x, norm1, w_in, w_fourier, b_fourier, q_norm, k_norm, sink, g_fourier_out, g_attn_out,
           w_o, norm2, w_up, conv_w, conv_b, w_down):
    b, s, d = x.shape
    depth = norm1.shape[0]
    fw = w_fourier.shape[-1]
    qc = g_attn_out.shape[-1]
    assert w_in.shape[-1] == fw + qc + 2 * N_KV_HEADS * HEAD_DIM and N_KV_HEADS * HEAD_DIM == LANES
    tm = min(TOKEN_TILE, s)
    big = min(2 * tm, s)

    tw = _seq_dft_matrices(s)
    cdft = (_channel_dft(fw) * (1.0 / math.sqrt(s * fw))).astype(BF16)
    cos_t, sin_t = _rope_tables(s)
    lane = jnp.arange(2 * LANES)
    same_head = lane[:, None] // HEAD_DIM == lane[None, :] // HEAD_DIM
    seg = jnp.where(same_head, 1.0 / HEAD_DIM, 0.0).astype(BF16)
    bias = _band_bias()
    reps = LANES // HEAD_DIM

    row = lambda a: a.reshape(depth, 1, a.shape[-1])
    g1, g2, bf, gf, ga, cb = map(row, (norm1, norm2, b_fourier, g_fourier_out, g_attn_out, conv_b))
    qg = row(jnp.tile(q_norm * (LOG2E / math.sqrt(HEAD_DIM)), (1, reps)))
    kg = row(jnp.tile(k_norm, (1, reps)))
    w_in, w_fourier, w_o, w_up, w_down = (w.astype(BF16) for w in (w_in, w_fourier, w_o, w_up, w_down))

    for l in range(depth):
        ab, q, k2, v2 = _in_proj(x, g1, w_in, cdft, qg, kg, cos_t, sin_t, seg,
                                 layer=l, fw=fw, qc=qc, tm=big, sub=tm // 2)
        f = _seq_dft(tw, ab, fw=fw)
        ya = _attention(q, k2, v2, bias, sink, layer=l, tq=big)
        x = _out_proj(x, f, ya, w_fourier, bf, gf, ga, w_o, layer=l, tm=tm)
        x = _conv_ffn(x, g2, w_up, conv_w, cb, w_down, layer=l, ts=tm, cw=256)
    return x
```

```python
import functools
import math

import jax
import jax.numpy as jnp
from jax import lax
from jax.experimental import pallas as pl
from jax.experimental.pallas import tpu as pltpu

HEAD_DIM = 64
N_KV_HEADS = 2
WINDOW = 128
BLOCK = 128
ROPE_THETA = 10000.0
EPS = 1e-6
LOG2E = math.log2(math.e)
LANES = 128
SUBLANES = 8
TWIDDLE_ROWS = 64
TOKEN_TILE = 512
SCORE_LOOKAHEAD = 4
VMEM_LIMIT = 56 * 1024 * 1024

F32 = jnp.float32
BF16 = jnp.bfloat16


def _dot(a, b):
    return jnp.dot(a, b, preferred_element_type=F32)


def _dot_nt(a, b):
    return lax.dot_general(a, b, (((1,), (1,)), ((), ())), preferred_element_type=F32)


def _rms_scale(xf, width):
    return lax.rsqrt(jnp.sum(xf * xf, axis=-1, keepdims=True) * (1.0 / width) + EPS)


def _params(sem):
    return pltpu.CompilerParams(dimension_semantics=sem, vmem_limit_bytes=VMEM_LIMIT)


def _const_spec(shape):
    nd = len(shape)
    return pl.BlockSpec(shape, lambda *_: (0,) * nd, pipeline_mode=pl.Buffered(1))


def _layer_spec(shape, layer):
    nd = len(shape)
    return pl.BlockSpec((None,) + tuple(shape), lambda *_: (layer,) + (0,) * nd,
                        pipeline_mode=pl.Buffered(1))


RADIX = 4


def _twiddle_kernel(base_c_ref, base_s_ref, blk_c_ref, blk_s_ref, tw_c_ref, tw_s_ref, out_ref):
    b = pl.program_id(0)
    n4 = base_c_ref.shape[1]
    c0 = blk_c_ref[pl.ds(b, 1), :]
    s0 = blk_s_ref[pl.ds(b, 1), :]
    bc = base_c_ref[...]
    bs = base_s_ref[...]
    c4 = c0 * bc - s0 * bs
    s4 = s0 * bc + c0 * bs
    for j in range(RADIX):
        cj = tw_c_ref[j:j + 1, :]
        sj = tw_s_ref[j:j + 1, :]
        out_ref[j, :, :n4] = (c4 * cj - s4 * sj).astype(BF16)
        out_ref[j, :, n4:] = (-(s4 * cj + c4 * sj)).astype(BF16)


def _dft_tables(n, rows, cols):
    m = (rows[:, None] * jnp.arange(cols, dtype=jnp.int32)[None, :]) % n
    ang = m.astype(F32) * (2.0 * math.pi / n)
    return jnp.cos(ang), jnp.sin(ang)


def _seq_dft_matrices(s):
    r = TWIDDLE_ROWS
    n4 = s // RADIX
    nblk = n4 // r
    base_c, base_s = _dft_tables(n4, jnp.arange(r, dtype=jnp.int32), n4)
    blk_c, blk_s = _dft_tables(n4, jnp.arange(nblk, dtype=jnp.int32) * r, n4)
    tw_c, tw_s = _dft_tables(s, jnp.arange(RADIX, dtype=jnp.int32), n4)
    return pl.pallas_call(
        _twiddle_kernel,
        grid=(nblk,),
        in_specs=[_const_spec((r, n4)), _const_spec((r, n4)),
                  _const_spec((nblk, n4)), _const_spec((nblk, n4)),
                  _const_spec((RADIX, n4)), _const_spec((RADIX, n4))],
        out_specs=pl.BlockSpec((RADIX, r, 2 * n4), lambda i: (0, i, 0)),
        out_shape=jax.ShapeDtypeStruct((RADIX, n4, 2 * n4), BF16),
        compiler_params=_params(("arbitrary",)),
        name="twiddle",
    )(base_c, base_s, blk_c, blk_s, tw_c, tw_s)


def _head_meansq(t, seg_mean):
    return _dot((t * t).astype(BF16), seg_mean)


def _head_norm_rope(t, ms, gain, cos, sin_signed):
    tn = t * lax.rsqrt(ms + EPS) * gain
    lane = lax.broadcasted_iota(jnp.int32, tn.shape, 1)
    half = HEAD_DIM // 2
    rot = jnp.where(lane % HEAD_DIM < half,
                    pltpu.roll(tn, LANES - half, axis=1),
                    pltpu.roll(tn, half, axis=1))
    return tn * cos + rot * sin_signed


def _split_heads(t, fill):
    lane = lax.broadcasted_iota(jnp.int32, t.shape, 1)
    sw = pltpu.roll(t, HEAD_DIM, axis=1)
    first = lane < HEAD_DIM
    if fill is None:
        return jnp.where(first, t, sw), jnp.where(first, sw, t)
    return jnp.where(first, t, fill), jnp.where(first, sw, fill)


def _in_proj_kernel(fw, qc, sub, x_ref, g1_ref, w_in_ref, cdft_ref, qg_ref, kg_ref,
                    cos_ref, sin_ref, seg_ref, ab_ref, q_ref, k_ref, v_ref):
    d = x_ref.shape[-1]
    seg = seg_ref[...]
    sw = seg.shape[0]

    def project(rows):
        x = x_ref[0, rows, :]
        h = (x * _rms_scale(x, d) * g1_ref[...]).astype(BF16)
        return _dot(h, w_in_ref[...])

    def finish(rows, p):
        ab_ref[0, rows, :] = _dot(p[:, :fw].astype(BF16), cdft_ref[...]).astype(BF16)
        cos = cos_ref[rows, :]
        sin = sin_ref[rows, :]
        for c0 in range(0, qc, sw):
            ms = _head_meansq(p[:, fw + c0: fw + c0 + sw], seg)
            for c in range(c0, c0 + sw, LANES):
                t = p[:, fw + c: fw + c + LANES]
                r = _head_norm_rope(t, ms[:, c - c0:c - c0 + LANES], qg_ref[...], cos, sin)
                q_ref[0, rows, c:c + LANES] = r.astype(BF16)
        kcol = p[:, fw + qc: fw + qc + LANES]
        kr = _head_norm_rope(kcol, _head_meansq(kcol, seg[:LANES, :LANES]), kg_ref[...], cos, sin)
        k0, k1 = _split_heads(kr, None)
        k_ref[0, rows, :LANES] = k0.astype(BF16)
        k_ref[0, rows, LANES:] = k1.astype(BF16)
        v0, v1 = _split_heads(p[:, fw + qc + LANES: fw + qc + 2 * LANES], 1.0)
        v_ref[0, rows, :LANES] = v0.astype(BF16)
        v_ref[0, rows, LANES:] = v1.astype(BF16)

    subs = [pl.ds(r0, sub) for r0 in range(0, x_ref.shape[1], sub)]
    p = project(subs[0])
    for idx, rows in enumerate(subs):
        nxt = project(subs[idx + 1]) if idx + 1 < len(subs) else None
        finish(rows, p)
        p = nxt


def _in_proj(x, g1, w_in, cdft, qg, kg, cos_t, sin_t, seg, *, layer, fw, qc, tm, sub):
    b, s, d = x.shape
    in_cols = w_in.shape[-1]
    kernel = functools.partial(_in_proj_kernel, fw, qc, sub)
    tok = lambda w: pl.BlockSpec((1, tm, w), lambda bi, i: (bi, i, 0))
    return pl.pallas_call(
        kernel,
        grid=(b, s // tm),
        in_specs=[tok(d), _layer_spec((1, d), layer), _layer_spec((d, in_cols), layer),
                  _const_spec((fw, 2 * fw)), _layer_spec((1, LANES), layer),
                  _layer_spec((1, LANES), layer),
                  pl.BlockSpec((tm, LANES), lambda bi, i: (i, 0)),
                  pl.BlockSpec((tm, LANES), lambda bi, i: (i, 0)),
                  _const_spec(seg.shape)],
        out_specs=[tok(2 * fw), tok(qc), tok(2 * LANES), tok(2 * LANES)],
        out_shape=[jax.ShapeDtypeStruct((b, s, 2 * fw), BF16),
                   jax.ShapeDtypeStruct((b, s, qc), BF16),
                   jax.ShapeDtypeStruct((b, s, 2 * LANES), BF16),
                   jax.ShapeDtypeStruct((b, s, 2 * LANES), BF16)],
        compiler_params=_params(("parallel", "parallel")),
        name="in_proj",
    )(x, g1, w_in, cdft, qg, kg, cos_t, sin_t, seg)


def _seq_dft_kernel(fw, rc, tw_ref, ab_ref, f_ref, pr_ref):
    j = pl.program_id(1)
    n4 = tw_ref.shape[1]

    @pl.when(j == 0)
    def _():
        def body(r, carry):
            def ld(q, lo):
                return ab_ref[0, pl.ds(pl.multiple_of(q * n4 + r * rc, rc), rc), lo:lo + fw].astype(F32)
            a0, a1, a2, a3 = (ld(q, 0) for q in range(RADIX))
            b0, b1, b2, b3 = (ld(q, fw) for q in range(RADIX))
            sa02, da02, sa13, da13 = a0 + a2, a0 - a2, a1 + a3, a1 - a3
            sb02, db02, sb13, db13 = b0 + b2, b0 - b2, b1 + b3, b1 - b3
            p = (sa02 + sa13, da02 - db13, sa02 - sa13, da02 + db13)
            rr = (sb02 + sb13, db02 + da13, sb02 - sb13, db02 - da13)
            rows = pl.ds(pl.multiple_of(r * rc, rc), rc)
            rows_r = pl.ds(pl.multiple_of(n4 + r * rc, rc), rc)
            for jj in range(RADIX):
                pr_ref[jj, rows, :] = p[jj].astype(BF16)
                pr_ref[jj, rows_r, :] = rr[jj].astype(BF16)
            return carry
        lax.fori_loop(0, n4 // rc, body, 0)

    f_ref[0] = _dot(tw_ref[0], pr_ref[j]).astype(f_ref.dtype)


def _seq_dft(tw, ab, *, fw):
    b, s, _ = ab.shape
    n4 = s // RADIX
    rc = min(128, n4)
    return pl.pallas_call(
        functools.partial(_seq_dft_kernel, fw, rc),
        grid=(b, RADIX),
        in_specs=[pl.BlockSpec((1, n4, 2 * n4), lambda bi, j: (j, 0, 0)),
                  pl.BlockSpec((1, s, 2 * fw), lambda bi, j: (bi, 0, 0))],
        out_specs=pl.BlockSpec((1, n4, fw), lambda bi, j: (bi, 0, j)),
        out_shape=jax.ShapeDtypeStruct((b, n4, RADIX * fw), BF16),
        scratch_shapes=[pltpu.VMEM((RADIX, 2 * n4, fw), BF16)],
        compiler_params=_params(("parallel", "arbitrary")),
        name="seq_dft",
    )(tw, ab)


def _attention_kernel(layer, nblk_step, s, q_ref, k_ref, v_ref, bias_ref, sink_ref, o_ref):
    j = pl.program_id(1)
    band = 3 * BLOCK
    lane = lax.broadcasted_iota(jnp.int32, (BLOCK, LANES), 1)
    first = lane < HEAD_DIM
    n_pairs = q_ref.shape[-1] // LANES
    pairs_per_kv = n_pairs // N_KV_HEADS

    def band_start(ib):
        q0 = (j * nblk_step + ib) * BLOCK
        return q0, pl.multiple_of(jnp.clip(q0 - BLOCK, 0, s - band), BLOCK)

    def scores(ib, pr):
        q0, start = band_start(ib)
        kh = pr // pairs_per_kv
        kk = k_ref[0, pl.ds(start, band), kh * LANES:(kh + 1) * LANES]
        q2 = q_ref[0, ib * BLOCK:(ib + 1) * BLOCK, pr * LANES:(pr + 1) * LANES]
        zero = jnp.zeros_like(q2)
        lhs = jnp.concatenate([jnp.where(first, q2, zero), jnp.where(first, zero, q2)], axis=0)
        bias = bias_ref[(q0 - start) // BLOCK]
        return _dot_nt(lhs, kk) + jnp.concatenate([bias, bias], axis=0)

    def finish(ib, pr, sc):
        _, start = band_start(ib)
        kh = pr // pairs_per_kv
        vv = v_ref[0, pl.ds(start, band), kh * LANES:(kh + 1) * LANES]
        mx = jnp.max(sc, axis=-1, keepdims=True)
        probs, sink_p = [], []
        for hh in range(2):
            snk = sink_ref[layer, 2 * pr + hh] * LOG2E
            m = jnp.maximum(mx[hh * BLOCK:(hh + 1) * BLOCK], snk)
            probs.append(jnp.exp2(sc[hh * BLOCK:(hh + 1) * BLOCK] - m).astype(BF16))
            sink_p.append(jnp.exp2(snk - m))
        pv = _dot(jnp.concatenate(probs, axis=0), vv)
        pa, pb = pv[:BLOCK], pv[BLOCK:]
        ra, rb = pltpu.roll(pa, HEAD_DIM, axis=1), pltpu.roll(pb, HEAD_DIM, axis=1)
        out_a = pa / (ra + sink_p[0])
        out_b = rb / (pb + sink_p[1])
        o_ref[0, ib * BLOCK:(ib + 1) * BLOCK, pr * LANES:(pr + 1) * LANES] = jnp.where(
            first, out_a, out_b).astype(o_ref.dtype)

    work = [(ib, pr) for ib in range(nblk_step) for pr in range(n_pairs)]
    pending = [scores(*item) for item in work[:SCORE_LOOKAHEAD]]
    for idx, item in enumerate(work):
        if idx + SCORE_LOOKAHEAD < len(work):
            pending.append(scores(*work[idx + SCORE_LOOKAHEAD]))
        finish(*item, pending.pop(0))


def _band_bias():
    i = jnp.arange(BLOCK)[None, :, None]
    jj = jnp.arange(3 * BLOCK)[None, None, :]
    off = jnp.arange(3)[:, None, None] * BLOCK
    return jnp.where(jnp.abs(jj - off - i) <= WINDOW, 0.0, -jnp.inf).astype(F32)


def _attention(q, k2, v2, bias, sink, *, layer, tq):
    b, s, qc = q.shape
    nblk_step = tq // BLOCK
    return pl.pallas_call(
        functools.partial(_attention_kernel, layer, nblk_step, s),
        grid=(b, s // tq),
        in_specs=[pl.BlockSpec((1, tq, qc), lambda bi, i: (bi, i, 0)),
                  pl.BlockSpec((1, s, k2.shape[-1]), lambda bi, i: (bi, 0, 0)),
                  pl.BlockSpec((1, s, v2.shape[-1]), lambda bi, i: (bi, 0, 0)),
                  _const_spec(bias.shape),
                  pl.BlockSpec(memory_space=pltpu.MemorySpace.SMEM)],
        out_specs=pl.BlockSpec((1, tq, qc), lambda bi, i: (bi, i, 0)),
        out_shape=jax.ShapeDtypeStruct((b, s, qc), BF16),
        compiler_params=_params(("parallel", "parallel")),
        name="attention",
    )(q, k2, v2, bias, sink)


def _out_proj_kernel(sub, x_ref, f_ref, ya_ref, wf_ref, bf_ref, gf_ref, ga_ref, wo_ref, o_ref):
    fw = wf_ref.shape[0]

    def fourier(r0):
        f4 = f_ref[0, r0 // RADIX:(r0 + sub) // RADIX, :]
        f = jnp.stack([f4[:, j * fw:(j + 1) * fw] for j in range(RADIX)], axis=1).reshape(sub, fw)
        return _dot(f, wf_ref[...])

    def finish(r0, yf):
        rows = pl.ds(r0, sub)
        yf = yf + bf_ref[...]
        nf = (yf * _rms_scale(yf, fw) * gf_ref[...]).astype(BF16)
        ya = ya_ref[0, rows, :].astype(F32)
        na = (ya * _rms_scale(ya, ya.shape[-1]) * ga_ref[...]).astype(BF16)
        o_ref[0, rows, :] = x_ref[0, rows, :] + _dot(nf, wo_ref[:fw, :]) + _dot(na, wo_ref[fw:, :])

    starts = list(range(0, x_ref.shape[1], sub))
    yf = fourier(starts[0])
    for idx, r0 in enumerate(starts):
        nxt = fourier(starts[idx + 1]) if idx + 1 < len(starts) else None
        finish(r0, yf)
        yf = nxt


def _out_proj(x, f, ya, wf, bf, gf, ga, wo, *, layer, tm, sub):
    b, s, d = x.shape
    fw = wf.shape[-1]
    qc = ya.shape[-1]
    tok = lambda w: pl.BlockSpec((1, tm, w), lambda bi, i: (bi, i, 0))
    return pl.pallas_call(
        functools.partial(_out_proj_kernel, sub),
        grid=(b, s // tm),
        in_specs=[tok(d), pl.BlockSpec((1, tm // RADIX, RADIX * fw), lambda bi, i: (bi, i, 0)),
                  tok(qc), _layer_spec((fw, fw), layer),
                  _layer_spec((1, fw), layer), _layer_spec((1, fw), layer),
                  _layer_spec((1, qc), layer), _layer_spec((fw + qc, d), layer)],
        out_specs=tok(d),
        out_shape=jax.ShapeDtypeStruct((b, s, d), F32),
        compiler_params=_params(("parallel", "parallel")),
        name="out_proj",
    )(x, f, ya, wf, bf, gf, ga, wo)


def _shift_rows(t, edge_prev, edge_next):
    rows = t.shape[0]
    sub = lax.broadcasted_iota(jnp.int32, (SUBLANES, t.shape[1]), 0)
    dn = pltpu.roll(t, 1, axis=0)
    dn = jnp.concatenate([jnp.where(sub == 0, edge_prev, dn[:SUBLANES]), dn[SUBLANES:]], axis=0)
    up = pltpu.roll(t, rows - 1, axis=0)
    up = jnp.concatenate([up[:rows - SUBLANES],
                          jnp.where(sub == SUBLANES - 1, edge_next, up[rows - SUBLANES:])], axis=0)
    return dn, up


def _conv_ffn_kernel(d_ff, cw, x_ref, xp_ref, xn_ref, g2_ref, wup_ref, cwt_ref, cb_ref,
                     wdn_ref, o_ref, hext_ref, act_ref):
    i = pl.program_id(1)
    last = pl.num_programs(1) - 1
    x = x_ref[0]
    ts, d = x.shape
    g2 = g2_ref[...]
    hext_ref[:ts, :] = (x * _rms_scale(x, d) * g2).astype(BF16)
    xe = jnp.concatenate([xp_ref[0], xn_ref[0]], axis=0)
    erow = lax.broadcasted_iota(jnp.int32, (2 * SUBLANES, 1), 0)
    keep = jnp.where(erow < SUBLANES, (i > 0).astype(F32), (i < last).astype(F32))
    hext_ref[ts:, :] = (xe * (_rms_scale(xe, d) * keep) * g2).astype(BF16)

    def conv(col):
        r = _dot(hext_ref[...], wup_ref[:, col:col + cw])
        t = r[:ts]
        dn, up = _shift_rows(t, r[ts + SUBLANES - 1:ts + SUBLANES], r[ts + SUBLANES:ts + SUBLANES + 1])
        cwt = cwt_ref[:, col:col + cw]
        return dn * cwt[0:1] + t * cwt[1:2] + up * cwt[2:3] + cb_ref[:, col:col + cw]

    for c in range(d_ff // cw):
        gate = conv(c * cw)
        val = conv(d_ff + c * cw)
        act_ref[:, c * cw:(c + 1) * cw] = (gate / (1.0 + jnp.exp(-gate)) * val).astype(BF16)
    o_ref[0] = x + _dot(act_ref[...], wdn_ref[...])


def _conv_ffn(x, g2, wup, cwt, cb, wdn, *, layer, ts, cw):
    b, s, d = x.shape
    d_ff = wdn.shape[-2]
    nsub = ts // SUBLANES
    nsub_total = s // SUBLANES
    return pl.pallas_call(
        functools.partial(_conv_ffn_kernel, d_ff, cw),
        grid=(b, s // ts),
        in_specs=[pl.BlockSpec((1, ts, d), lambda bi, i: (bi, i, 0)),
                  pl.BlockSpec((1, SUBLANES, d),
                               lambda bi, i: (bi, jnp.maximum(i * nsub - 1, 0), 0)),
                  pl.BlockSpec((1, SUBLANES, d),
                               lambda bi, i: (bi, jnp.minimum((i + 1) * nsub, nsub_total - 1), 0)),
                  _layer_spec((1, d), layer), _layer_spec((d, 2 * d_ff), layer),
                  _layer_spec((cwt.shape[-2], 2 * d_ff), layer), _layer_spec((1, 2 * d_ff), layer),
                  _layer_spec((d_ff, d), layer)],
        out_specs=pl.BlockSpec((1, ts, d), lambda bi, i: (bi, i, 0)),
        out_shape=jax.ShapeDtypeStruct((b, s, d), F32),
        scratch_shapes=[pltpu.VMEM((ts + 2 * SUBLANES, d), BF16), pltpu.VMEM((ts, d_ff), BF16)],
        compiler_params=_params(("parallel", "parallel")),
        name="conv_ffn",
    )(x, x, x, g2, wup, cwt, cb, wdn)


def _channel_dft(c):
    cc, cs = _dft_tables(c, jnp.arange(c, dtype=jnp.int32), c)
    return jnp.concatenate([cc, cs], axis=1)


def _rope_tables(s):
    inv_freq = 1.0 / (ROPE_THETA ** (jnp.arange(0, HEAD_DIM, 2, dtype=F32) / HEAD_DIM))
    ang = jnp.arange(s, dtype=F32)[:, None] * inv_freq[None, :]
    cos, sin = jnp.cos(ang), jnp.sin(ang)
    reps = LANES // HEAD_DIM
    return (jnp.tile(jnp.concatenate([cos, cos], axis=1), (1, reps)),
            jnp.tile(jnp.concatenate([-sin, sin], axis=1), (1, reps)))


def kernel(x, norm1, w_in, w_fourier, b_fourier, q_norm, k_norm, sink, g_fourier_out, g_attn_out,
           w_o, norm2, w_up, conv_w, conv_b, w_down):
    b, s, d = x.shape
    depth = norm1.shape[0]
    fw = w_fourier.shape[-1]
    qc = g_attn_out.shape[-1]
    assert w_in.shape[-1] == fw + qc + 2 * N_KV_HEADS * HEAD_DIM and N_KV_HEADS * HEAD_DIM == LANES
    tm = min(TOKEN_TILE, s)
    big = min(2 * tm, s)

    tw = _seq_dft_matrices(s)
    cdft = (_channel_dft(fw) * (1.0 / math.sqrt(s * fw))).astype(BF16)
    cos_t, sin_t = _rope_tables(s)
    lane = jnp.arange(2 * LANES)
    same_head = lane[:, None] // HEAD_DIM == lane[None, :] // HEAD_DIM
    seg = jnp.where(same_head, 1.0 / HEAD_DIM, 0.0).astype(BF16)
    bias = _band_bias()
    reps = LANES // HEAD_DIM

    row = lambda a: a.reshape(depth, 1, a.shape[-1])
    g1, g2, bf, gf, ga, cb = map(row, (norm1, norm2, b_fourier, g_fourier_out, g_attn_out, conv_b))
    qg = row(jnp.tile(q_norm * (LOG2E / math.sqrt(HEAD_DIM)), (1, reps)))
    kg = row(jnp.tile(k_norm, (1, reps)))
    w_in, w_fourier, w_o, w_up, w_down = (w.astype(BF16) for w in (w_in, w_fourier, w_o, w_up, w_down))

    for l in range(depth):
        ab, q, k2, v2 = _in_proj(x, g1, w_in, cdft, qg, kg, cos_t, sin_t, seg,
                                 layer=l, fw=fw, qc=qc, tm=big, sub=tm // 2)
        f = _seq_dft(tw, ab, fw=fw)
        ya = _attention(q, k2, v2, bias, sink, layer=l, tq=big)
        x = _out_proj(x, f, ya, w_fourier, bf, gf, ga, w_o, layer=l, tm=big, sub=tm)
        x = _conv_ffn(x, g2, w_up, conv_w, cb, w_down, layer=l, ts=tm, cw=256)
    return x
```
